```python
import math
import jax, jax.numpy as jnp
from jax import lax
import numpy as np

D_MODEL = 1024
BATCH = 16
SEQ = 2048
DEPTH = 1
DEC_BATCH = 2
DEC_SEQ = 16384
PAST_LEN = 128

N_META = 16
HEAD_DIM = 64
N_HEADS = 8
N_KV_HEADS = 2
GQA_GROUP = N_HEADS // N_KV_HEADS
W_ATT = N_HEADS * HEAD_DIM
W_KV = N_KV_HEADS * HEAD_DIM
W_CONV = 512
D_MIX = W_ATT + W_CONV
WINDOW = 128
BLOCK = 128
CONV_K = 31
CONV_PAD = CONV_K // 2
NORM_EPS = 1e-6
LN_EPS = 1e-5
SPLIT_SIZES = (W_ATT, W_KV, W_KV, W_ATT, W_CONV, W_CONV, W_CONV)
IN_DIM = sum(SPLIT_SIZES)
SPLIT_IDX = tuple(int(i) for i in np.cumsum(SPLIT_SIZES)[:-1])

kernel_name = "hymba_conformer_swa_encoder"


def rmsnorm(x, w):
    xf = x.astype(jnp.float32)
    y = xf * lax.rsqrt(jnp.mean(xf * xf, axis=-1, keepdims=True) + NORM_EPS)
    return (y * w.astype(jnp.float32)).astype(x.dtype)


def layernorm(x, w, b):
    xf = x.astype(jnp.float32)
    mu = jnp.mean(xf, axis=-1, keepdims=True)
    var = jnp.mean(jnp.square(xf - mu), axis=-1, keepdims=True)
    y = (xf - mu) * lax.rsqrt(var + LN_EPS)
    return (y * w.astype(jnp.float32) + b.astype(jnp.float32)).astype(x.dtype)


def alibi_slopes():
    h = jnp.arange(1, N_HEADS + 1, dtype=jnp.float32)
    return jnp.exp2(-8.0 * h / N_HEADS).reshape(N_KV_HEADS, GQA_GROUP, 1, 1)


def windowed_gqa(q, k, v, sink):
    B, L, H, hd = q.shape
    S = L - N_META
    nb = S // BLOCK
    scale = hd ** -0.5
    slopes = alibi_slopes()
    qg = q.reshape(B, L, N_KV_HEADS, GQA_GROUP, hd)
    qm, qr = qg[:, :N_META], qg[:, N_META:]
    km, kr = k[:, :N_META], k[:, N_META:]
    vm, vr = v[:, :N_META], v[:, N_META:]

    qb = qr.reshape(B, nb, BLOCK, N_KV_HEADS, GQA_GROUP, hd)
    pad = ((0, 0), (BLOCK, BLOCK), (0, 0), (0, 0))
    kp = jnp.pad(kr, pad).reshape(B, nb + 2, BLOCK, N_KV_HEADS, hd)
    vp = jnp.pad(vr, pad).reshape(B, nb + 2, BLOCK, N_KV_HEADS, hd)
    kw = jnp.concatenate([kp[:, :-2], kp[:, 1:-1], kp[:, 2:]], axis=2)
    vw = jnp.concatenate([vp[:, :-2], vp[:, 1:-1], vp[:, 2:]], axis=2)

    rel = jnp.arange(3 * BLOCK)[None, :] - BLOCK - jnp.arange(BLOCK)[:, None]
    dist = jnp.abs(rel).astype(jnp.float32)
    key_idx = jnp.arange(nb)[:, None] * BLOCK + jnp.arange(3 * BLOCK)[None, :] - BLOCK
    in_range = (key_idx >= 0) & (key_idx < S)
    valid = (jnp.abs(rel) <= WINDOW)[None] & in_range[:, None, :]

    s_win = jnp.einsum('bnqkgd,bnskd->bnkgqs', qb, kw,
                       preferred_element_type=jnp.float32) * scale
    s_win = s_win - (slopes * dist)[None, None]
    s_win = jnp.where(valid[None, :, None, None], s_win, -jnp.inf)
    s_meta = jnp.einsum('bnqkgd,bmkd->bnkgqm', qb, km,
                        preferred_element_type=jnp.float32) * scale
    sink_f = sink.astype(jnp.float32).reshape(N_KV_HEADS, GQA_GROUP, 1, 1)
    sink_b = jnp.broadcast_to(sink_f, (B, nb, N_KV_HEADS, GQA_GROUP, BLOCK, 1))
    p = jax.nn.softmax(jnp.concatenate([sink_b, s_meta, s_win], axis=-1), axis=-1)
    p = p.astype(v.dtype)
    o = (jnp.einsum('bnkgqm,bmkd->bnqkgd', p[..., 1:1 + N_META], vm)
         + jnp.einsum('bnkgqs,bnskd->bnqkgd', p[..., 1 + N_META:], vw))
    o_real = o.reshape(B, S, H, hd)

    kh = k[:, :N_META + BLOCK]
    vh = v[:, :N_META + BLOCK]
    rel_h = jnp.arange(N_META + BLOCK)[None, :] - jnp.arange(N_META)[:, None]
    dist_h = jnp.abs(rel_h).astype(jnp.float32)
    s_h = jnp.einsum('btkgd,bskd->bkgts', qm, kh,
                     preferred_element_type=jnp.float32) * scale
    s_h = s_h - slopes * dist_h
    s_h = jnp.where((jnp.abs(rel_h) <= WINDOW)[None, None, None], s_h, -jnp.inf)
    sink_h = jnp.broadcast_to(sink_f, (B, N_KV_HEADS, GQA_GROUP, N_META, 1))
    p_h = jax.nn.softmax(jnp.concatenate([sink_h, s_h], axis=-1), axis=-1).astype(v.dtype)
    o_meta = jnp.einsum('bkgts,bskd->btkgd', p_h[..., 1:], vh).reshape(B, N_META, H, hd)
    return jnp.concatenate([o_meta, o_real], axis=1)


def conformer_conv(c_val, c_glu, conv_w, conv_b, ln_w, ln_b):
    u = c_val * jax.nn.sigmoid(c_glu)
    C = u.shape[-1]
    y = lax.conv_general_dilated(
        u, conv_w.astype(u.dtype)[:, None, :], window_strides=(1,),
        padding=[(CONV_PAD, CONV_PAD)], dimension_numbers=('NWC', 'WIO', 'NWC'),
        feature_group_count=C) + conv_b.astype(u.dtype)
    return jax.nn.silu(layernorm(y, ln_w, ln_b))


def hybrid_layer(z, norm_w, w_in, q_norm_w, k_norm_w, sink, conv_w, conv_b, ln_w, ln_b, w_out):
    B, L, _ = z.shape
    h = rmsnorm(z, norm_w)
    proj = jnp.einsum('bld,de->ble', h, w_in)
    q, k, v, g_att, c_val, c_glu, g_conv = jnp.split(proj, SPLIT_IDX, axis=-1)
    q = rmsnorm(q.reshape(B, L, N_HEADS, HEAD_DIM), q_norm_w)
    k = rmsnorm(k.reshape(B, L, N_KV_HEADS, HEAD_DIM), k_norm_w)
    v = v.reshape(B, L, N_KV_HEADS, HEAD_DIM)
    att = windowed_gqa(q, k, v, sink).reshape(B, L, W_ATT) * jax.nn.silu(g_att)
    cnv = conformer_conv(c_val, c_glu, conv_w, conv_b, ln_w, ln_b) * jax.nn.silu(g_conv)
    mixed = jnp.concatenate([att, cnv], axis=-1)
    return z + jnp.einsum('ble,ed->bld', mixed, w_out)


def setup_inputs(seed: int = 0) -> dict:
    key = jax.random.key(seed)
    ks = jax.random.split(key, 16)
    f32 = jnp.float32
    return {
        "x_prompt": jax.random.normal(ks[0], (BATCH, SEQ, D_MODEL), f32),
        "x_sample": jax.random.normal(ks[1], (DEC_BATCH, DEC_SEQ, D_MODEL), f32),
        "meta_tokens": jax.random.normal(ks[2], (N_META, D_MODEL), f32),
        "norm_w": 1.0 + 0.05 * jax.random.normal(ks[3], (DEPTH, D_MODEL), f32),
        "w_in": jax.random.normal(ks[4], (DEPTH, D_MODEL, IN_DIM), f32) * D_MODEL ** -0.5,
        "q_norm_w": 1.0 + 0.05 * jax.random.normal(ks[5], (DEPTH, HEAD_DIM), f32),
        "k_norm_w": 1.0 + 0.05 * jax.random.normal(ks[6], (DEPTH, HEAD_DIM), f32),
        "sink_logits": 0.5 * jax.random.normal(ks[7], (DEPTH, N_HEADS), f32),
        "conv_w": jax.random.normal(ks[8], (DEPTH, CONV_K, W_CONV), f32) * CONV_K ** -0.5,
        "conv_b": 0.01 * jax.random.normal(ks[9], (DEPTH, W_CONV), f32),
        "conv_ln_w": 1.0 + 0.05 * jax.random.normal(ks[10], (DEPTH, W_CONV), f32),
        "conv_ln_b": 0.01 * jax.random.normal(ks[11], (DEPTH, W_CONV), f32),
        "w_out": jax.random.normal(ks[12], (DEPTH, D_MIX, D_MODEL), f32) * D_MIX ** -0.5,
    }


def reference(x_prompt, x_sample, meta_tokens, norm_w, w_in, q_norm_w, k_norm_w, sink_logits,
              conv_w, conv_b, conv_ln_w, conv_ln_b, w_out):
    def run(x):
        B = x.shape[0]
        meta = jnp.broadcast_to(meta_tokens.astype(x.dtype)[None], (B, N_META, D_MODEL))
        z = jnp.concatenate([meta, x], axis=1)
        for l in range(DEPTH):
            z = hybrid_layer(z, norm_w[l], w_in[l], q_norm_w[l], k_norm_w[l], sink_logits[l],
                             conv_w[l], conv_b[l], conv_ln_w[l], conv_ln_b[l], w_out[l])
        return z[:, N_META:]

    y_prompt = run(x_prompt)
    y_sample = run(x_sample)
    return (y_prompt, y_sample)
```

```python
import functools

import numpy as np
import jax
import jax.numpy as jnp
from jax import lax
from jax.experimental import pallas as pl
from jax.experimental.pallas import tpu as pltpu

F32 = jnp.float32
BF16 = jnp.bfloat16

D_MODEL = 1024
N_META = 16
HEAD_DIM = 64
N_HEADS = 8
N_KV_HEADS = 2
W_ATT = N_HEADS * HEAD_DIM
W_KV = N_KV_HEADS * HEAD_DIM
W_CONV = 512
D_MIX = W_ATT + W_CONV
BLOCK = 128
CONV_K = 31
CONV_PAD = CONV_K // 2
NORM_EPS = 1e-6
LN_EPS = 1e-5
COL_Q, COL_KV, COL_GATT, COL_CONV, COL_GCONV = 0, 512, 768, 1280, 2304

LANES = 128
HALO_U = 16
N_WIN = 3 * BLOCK
N_KEYS = N_WIN + N_META
V_ROWS = HEAD_DIM + 16
TILE = 512
VMEM_LIMIT_BYTES = 56 * 1024 * 1024


def _rms_rows(x, w):
    ms = jnp.mean(x * x, axis=-1, keepdims=True)
    return x * lax.rsqrt(ms + NORM_EPS) * w


def _head_rms(x, gmat, w):
    ms = jnp.dot((x * x).astype(BF16), gmat, preferred_element_type=F32)
    return x * lax.rsqrt(ms + NORM_EPS) * w


def _kv_split(kn):
    lane = lax.broadcasted_iota(jnp.int32, kn.shape, 1)
    lo = lane < HEAD_DIM
    kr = pltpu.roll(kn, HEAD_DIM, axis=1)
    zero = jnp.zeros_like(kn)
    even = (jnp.where(lo, kn, zero), jnp.where(lo, kr, zero))
    odd = (jnp.where(lo, zero, kr), jnp.where(lo, zero, kn))
    return even, odd


def _meta_kernel(meta_ref, nw_ref, win_ref, g_ref, kw_ref, kn_ref, v_ref, u_ref):
    h = _rms_rows(meta_ref[...], nw_ref[...]).astype(BF16)
    kv = jnp.dot(h, win_ref[:, COL_KV:COL_KV + 2 * W_KV], preferred_element_type=F32)
    kn_ref[...] = _head_rms(kv[:, :W_KV], g_ref[:W_KV, :W_KV], kw_ref[...])
    v_ref[...] = kv[:, W_KV:]
    cc = jnp.dot(h, win_ref[:, COL_CONV:COL_CONV + 2 * W_CONV], preferred_element_type=F32)
    u_ref[...] = cc[:, :W_CONV] * jax.nn.sigmoid(cc[:, W_CONV:])


def _main_kernel(x_ref, xres_ref, nw_ref, win_ref, g_ref, qw_ref, kw_ref, kme_ref, kmo_ref, vmt_ref,
                 um_ref, bias_ref, sink_ref, cw_ref, cb_ref, lnw_ref, lnb_ref, wout_ref,
                 out_ref,
                 q_s, ke_s, ko_s, vt_s, ga_s, gc_s, u_s, kp_s, mixed_s, *, tile, n_tiles):
    T = tile
    nblk = T // BLOCK
    j = pl.program_id(1)
    k_next = BLOCK + T
    u_next = HALO_U + T

    @pl.when(j >= 1)
    def _shift():
        for g in range(N_KV_HEADS):
            ke_s[g, 0:T + BLOCK] = ke_s[g, T:2 * T + BLOCK]
            ko_s[g, 0:T + BLOCK] = ko_s[g, T:2 * T + BLOCK]
            vt_s[g, 0:HEAD_DIM, 0:T + BLOCK] = vt_s[g, 0:HEAD_DIM, T:2 * T + BLOCK]
        for r in range(0, T, BLOCK):
            u_s[r:r + BLOCK] = u_s[T + r:T + r + BLOCK]
        u_s[T:T + HALO_U] = u_s[2 * T:2 * T + HALO_U]

    @pl.when(j == 0)
    def _init():
        zk = jnp.zeros((BLOCK, LANES), BF16)
        for g in range(N_KV_HEADS):
            ke_s[g, T:T + BLOCK] = zk
            ko_s[g, T:T + BLOCK] = zk
            vt_s[g, 0:HEAD_DIM, T:T + BLOCK] = jnp.zeros((HEAD_DIM, BLOCK), BF16)
            vt_s[g, HEAD_DIM:V_ROWS, :] = jnp.ones((V_ROWS - HEAD_DIM, 2 * T + BLOCK), BF16)
        u_s[T:T + HALO_U] = um_ref[...]

    @pl.when(j < n_tiles)
    def _project():
        slot = j % 2
        h = _rms_rows(x_ref[0], nw_ref[...]).astype(BF16)

        def proj(c0, n):
            return jnp.dot(h, win_ref[:, c0:c0 + n], preferred_element_type=F32)

        q_s[slot] = _head_rms(proj(COL_Q, W_ATT), g_ref[...], qw_ref[...]).astype(BF16)
        kv = proj(COL_KV, 2 * W_KV)
        kn = _head_rms(kv[:, :W_KV], g_ref[:W_KV, :W_KV], kw_ref[...])
        even, odd = _kv_split(kn)
        vt = kv[:, W_KV:].T.astype(BF16)
        for g in range(N_KV_HEADS):
            ke_s[g, k_next:k_next + T] = even[g].astype(BF16)
            ko_s[g, k_next:k_next + T] = odd[g].astype(BF16)
            vt_s[g, 0:HEAD_DIM, k_next:k_next + T] = vt[g * HEAD_DIM:(g + 1) * HEAD_DIM]
        ga_s[slot] = jax.nn.silu(proj(COL_GATT, W_ATT))
        cc = proj(COL_CONV, 2 * W_CONV)
        u_s[u_next:u_next + T] = cc[:, :W_CONV] * jax.nn.sigmoid(cc[:, W_CONV:])
        gc_s[slot] = jax.nn.silu(proj(COL_GCONV, W_CONV))

    @pl.when(j == n_tiles)
    def _pad_right():
        u_s[u_next:u_next + HALO_U] = jnp.zeros((HALO_U, W_CONV), F32)

    @pl.when(j >= 1)
    def _finish():
        slot = (j + 1) % 2
        for n in range(nblk):
            r0 = n * BLOCK
            if n == 0:
                var = jnp.where(j == 1, 0, 1)
            elif n == nblk - 1:
                var = jnp.where(j == n_tiles, 2, 1)
            else:
                var = 1
            if nblk == 1:
                var = jnp.where(j == 1, 0, jnp.where(j == n_tiles, 2, 1))

            att_parts = []
            for g in range(N_KV_HEADS):
                kp_s[0:N_WIN] = ke_s[g, r0:r0 + N_WIN]
                kp_s[N_WIN:N_KEYS] = kme_ref[g]
                kp_s[N_KEYS:N_KEYS + N_WIN] = ko_s[g, r0:r0 + N_WIN]
                kp_s[N_KEYS + N_WIN:2 * N_KEYS] = kmo_ref[g]
                c0 = g * 2 * LANES
                q2 = jnp.concatenate([q_s[slot, r0:r0 + BLOCK, c0:c0 + LANES],
                                      q_s[slot, r0:r0 + BLOCK, c0 + LANES:c0 + 2 * LANES]], axis=0)
                s = lax.dot_general(kp_s[...], q2, (((1,), (1,)), ((), ())), preferred_element_type=F32)
                s = s + bias_ref[var, g]
                vwin = jnp.concatenate([vt_s[g, :, r0:r0 + N_WIN], vmt_ref[g]], axis=1)
                halves = []
                for eo in range(2):
                    se = s[eo * N_KEYS:(eo + 1) * N_KEYS]
                    sk = sink_ref[g, eo:eo + 1, :]
                    m = jnp.maximum(jnp.max(se, axis=0, keepdims=True), sk)
                    p = jnp.exp(se - m).astype(BF16)
                    o = jnp.dot(vwin, p, preferred_element_type=F32)
                    den = o[HEAD_DIM:HEAD_DIM + 1] + jnp.exp(sk - m)
                    halves.append(o[0:HEAD_DIM] * (1.0 / den))
                att_t = jnp.concatenate([halves[0][:, :LANES], halves[1][:, :LANES],
                                         halves[0][:, LANES:], halves[1][:, LANES:]], axis=0)
                att_parts.append(att_t.T)
            att = jnp.concatenate(att_parts, axis=1)
            mixed_s[r0:r0 + BLOCK, 0:W_ATT] = (att * ga_s[slot, r0:r0 + BLOCK]).astype(BF16)

            cols = []
            for c0 in range(0, W_CONV, LANES):
                acc = jnp.broadcast_to(cb_ref[:, c0:c0 + LANES], (BLOCK, LANES))
                for k in range(CONV_K):
                    acc = acc + cw_ref[k:k + 1, c0:c0 + LANES] * u_s[r0 + k + 1:r0 + k + 1 + BLOCK, c0:c0 + LANES]
                cols.append(acc)
            y = jnp.concatenate(cols, axis=1)
            mu = jnp.mean(y, axis=-1, keepdims=True)
            yc = y - mu
            var_y = jnp.mean(yc * yc, axis=-1, keepdims=True)
            yn = yc * lax.rsqrt(var_y + LN_EPS) * lnw_ref[...] + lnb_ref[...]
            mixed_s[r0:r0 + BLOCK, W_ATT:D_MIX] = (jax.nn.silu(yn) * gc_s[slot, r0:r0 + BLOCK]).astype(BF16)

        out_ref[0] = xres_ref[0] + jnp.dot(mixed_s[...], wout_ref[...], preferred_element_type=F32)


def _bias_table():
    r = np.arange(2 * N_KEYS)
    c = np.arange(2 * LANES)
    eo = (r // N_KEYS)[:, None]
    kk = (r % N_KEYS)[:, None]
    lane_tile = (c // LANES)[None, :]
    qi = (c % LANES)[None, :]
    rel = np.abs(kk - BLOCK - qi).astype(np.float32)
    out = np.zeros((3, N_KV_HEADS, 2 * N_KEYS, 2 * LANES), np.float32)
    for g in range(N_KV_HEADS):
        head = 4 * g + 2 * lane_tile + eo
        slope = np.exp2(-8.0 * (head + 1) / N_HEADS).astype(np.float32)
        base = np.where(rel <= BLOCK, -slope * rel, -np.inf).astype(np.float32)
        base = np.where(kk >= N_WIN, np.float32(0.0), base)
        out[1, g] = base
        out[0, g] = np.where(kk < BLOCK, -np.inf, base)
        out[2, g] = np.where((kk >= 2 * BLOCK) & (kk < N_WIN), -np.inf, base)
    return out


def _const_spec(shape):
    nd = len(shape)
    return pl.BlockSpec(shape, lambda b, j, _nd=nd: (0,) * _nd, pipeline_mode=pl.Buffered(1))


def _run_stream(x, consts, *, tile):
    B, S, D = x.shape
    assert D == D_MODEL and S % tile == 0 and tile % BLOCK == 0 and S // BLOCK >= 2
    n_tiles = S // tile
    T = tile
    body = functools.partial(_main_kernel, tile=tile, n_tiles=n_tiles)
    x_spec = pl.BlockSpec((1, T, D), lambda b, j: (b, jnp.minimum(j, n_tiles - 1), 0))
    lag_map = lambda b, j: (b, jnp.maximum(j - 1, 0), 0)
    in_specs = [x_spec, pl.BlockSpec((1, T, D), lag_map)] + [_const_spec(c.shape) for c in consts]
    scratch = [
        pltpu.VMEM((2, T, W_ATT), BF16),
        pltpu.VMEM((N_KV_HEADS, 2 * T + BLOCK, LANES), BF16),
        pltpu.VMEM((N_KV_HEADS, 2 * T + BLOCK, LANES), BF16),
        pltpu.VMEM((N_KV_HEADS, V_ROWS, 2 * T + BLOCK), BF16),
        pltpu.VMEM((2, T, W_ATT), F32),
        pltpu.VMEM((2, T, W_CONV), F32),
        pltpu.VMEM((2 * T + 2 * HALO_U, W_CONV), F32),
        pltpu.VMEM((2 * N_KEYS, LANES), BF16),
        pltpu.VMEM((T, D_MIX), BF16),
    ]
    return pl.pallas_call(
        body,
        grid=(B, n_tiles + 1),
        in_specs=in_specs,
        out_specs=pl.BlockSpec((1, T, D), lag_map),
        out_shape=jax.ShapeDtypeStruct((B, S, D), F32),
        scratch_shapes=scratch,
        compiler_params=pltpu.CompilerParams(
            dimension_semantics=("arbitrary", "arbitrary"),
            vmem_limit_bytes=VMEM_LIMIT_BYTES),
        name="hybrid_block",
    )(x, x, *consts)


def kernel(x_prompt, x_sample, meta_tokens, norm_w, w_in, q_norm_w, k_norm_w, sink_logits,
           conv_w, conv_b, conv_ln_w, conv_ln_b, w_out):
    nw = norm_w[0].reshape(1, D_MODEL)
    win = w_in[0].astype(BF16)
    wout = w_out[0].astype(BF16)
    qw = (jnp.tile(q_norm_w[0], N_HEADS) * (HEAD_DIM ** -0.5)).reshape(1, W_ATT)
    kw = jnp.tile(k_norm_w[0], N_KV_HEADS).reshape(1, W_KV)
    head_id = np.arange(W_ATT) // HEAD_DIM
    gmat = jnp.asarray((head_id[:, None] == head_id[None, :]).astype(np.float32) / HEAD_DIM, BF16)

    kn_m, v_m, u_m = pl.pallas_call(
        _meta_kernel,
        out_shape=(jax.ShapeDtypeStruct((N_META, W_KV), F32),
                   jax.ShapeDtypeStruct((N_META, W_KV), F32),
                   jax.ShapeDtypeStruct((N_META, W_CONV), F32)),
        name="meta_tokens",
    )(meta_tokens, nw, win, gmat, kw)

    zeros = jnp.zeros((N_META, HEAD_DIM), F32)
    k_heads = [kn_m[:, :HEAD_DIM], kn_m[:, HEAD_DIM:]]
    kme = jnp.stack([jnp.concatenate([k, zeros], axis=1) for k in k_heads]).astype(BF16)
    kmo = jnp.stack([jnp.concatenate([zeros, k], axis=1) for k in k_heads]).astype(BF16)
    ones = jnp.ones((V_ROWS - HEAD_DIM, N_META), F32)
    vmt = jnp.stack([jnp.concatenate([v_m[:, g * HEAD_DIM:(g + 1) * HEAD_DIM].T, ones], axis=0)
                     for g in range(N_KV_HEADS)]).astype(BF16)

    lane_tile = np.arange(2 * LANES) // LANES
    head_idx = np.stack([np.stack([4 * g + 2 * lane_tile + eo for eo in range(2)]) for g in range(N_KV_HEADS)])
    sink_rows = sink_logits[0][head_idx]

    cw = jnp.pad(conv_w[0], ((0, 1), (0, 0)))
    consts = (nw, win, gmat, qw, kw, kme, kmo, vmt, u_m, jnp.asarray(_bias_table()), sink_rows,
              cw, conv_b[0].reshape(1, W_CONV), conv_ln_w[0].reshape(1, W_CONV),
              conv_ln_b[0].reshape(1, W_CONV), wout)
    y_prompt = _run_stream(x_prompt, consts, tile=TILE)
    y_sample = _run_stream(x_sample, consts, tile=TILE)
    return (y_prompt, y_sample)
```

```python
import functools

import numpy as np
import jax
import jax.numpy as jnp
from jax import lax
from jax.experimental import pallas as pl
from jax.experimental.pallas import tpu as pltpu

F32 = jnp.float32
BF16 = jnp.bfloat16

D_MODEL = 1024
N_META = 16
HEAD_DIM = 64
N_HEADS = 8
N_KV_HEADS = 2
W_ATT = N_HEADS * HEAD_DIM
W_KV = N_KV_HEADS * HEAD_DIM
W_CONV = 512
D_MIX = W_ATT + W_CONV
BLOCK = 128
CONV_K = 31
CONV_PAD = CONV_K // 2
NORM_EPS = 1e-6
LN_EPS = 1e-5
COL_Q, COL_KV, COL_GATT, COL_CONV, COL_GCONV = 0, 512, 768, 1280, 2304

LANES = 128
SUBLANES = 8
CONV_ROWS = 64
CONV_SRC_ROWS = CONV_ROWS + 32
HALO_U = 16
N_WIN = 3 * BLOCK
N_KEYS = N_WIN + N_META
V_ROWS = HEAD_DIM + 16
TILE = 512
VMEM_LIMIT_BYTES = 56 * 1024 * 1024


def _rms_rows(x, w):
    ms = jnp.mean(x * x, axis=-1, keepdims=True)
    return x * lax.rsqrt(ms + NORM_EPS) * w


def _head_rms(x, gmat, w):
    ms = jnp.dot((x * x).astype(BF16), gmat, preferred_element_type=F32)
    return x * lax.rsqrt(ms + NORM_EPS) * w


def _kv_split(kn):
    lane = lax.broadcasted_iota(jnp.int32, kn.shape, 1)
    lo = lane < HEAD_DIM
    kr = pltpu.roll(kn, HEAD_DIM, axis=1)
    zero = jnp.zeros_like(kn)
    even = (jnp.where(lo, kn, zero), jnp.where(lo, kr, zero))
    odd = (jnp.where(lo, zero, kr), jnp.where(lo, zero, kn))
    return even, odd


def _meta_kernel(meta_ref, nw_ref, win_ref, g_ref, kw_ref, kn_ref, v_ref, u_ref):
    h = _rms_rows(meta_ref[...], nw_ref[...]).astype(BF16)
    kv = jnp.dot(h, win_ref[:, COL_KV:COL_KV + 2 * W_KV], preferred_element_type=F32)
    kn_ref[...] = _head_rms(kv[:, :W_KV], g_ref[:W_KV, :W_KV], kw_ref[...])
    v_ref[...] = kv[:, W_KV:]
    cc = jnp.dot(h, win_ref[:, COL_CONV:COL_CONV + 2 * W_CONV], preferred_element_type=F32)
    u_ref[...] = cc[:, :W_CONV] * jax.nn.sigmoid(cc[:, W_CONV:])


def _main_kernel(x_ref, xres_ref, nw_ref, win_ref, g_ref, qw_ref, kw_ref, kme_ref, kmo_ref, vmt_ref,
                 um_ref, bias_ref, sink_ref, cw_ref, cb_ref, lnw_ref, lnb_ref, wout_ref,
                 out_ref,
                 h_s, q_cur, q_new, ke_cur, ke_new, ko_cur, ko_new, vt_cur, vt_new,
                 ga_cur, ga_new, gc_cur, gc_new, u_cur, u_new, y_s, mixed_s, *, tile, tiles_per_seq, n_tiles):
    T = tile
    nblk = T // BLOCK
    i = pl.program_id(0)
    seq_first = (i - 1) % tiles_per_seq == 0
    seq_last = i % tiles_per_seq == 0

    def shift():
        for g in range(N_KV_HEADS):
            ke_cur[g, 0:BLOCK] = ke_cur[g, T:T + BLOCK]
            ko_cur[g, 0:BLOCK] = ko_cur[g, T:T + BLOCK]
            vt_cur[g, 0:HEAD_DIM, 0:BLOCK] = vt_cur[g, 0:HEAD_DIM, T:T + BLOCK]
            ke_cur[g, BLOCK:BLOCK + T] = ke_new[g]
            ko_cur[g, BLOCK:BLOCK + T] = ko_new[g]
            vt_cur[g, 0:HEAD_DIM, BLOCK:BLOCK + T] = vt_new[g, 0:HEAD_DIM]
        u_cur[0:HALO_U] = jnp.where(seq_first, um_ref[...], u_cur[T:T + HALO_U])
        for r in range(0, T, BLOCK):
            u_cur[HALO_U + r:HALO_U + r + BLOCK] = u_new[r:r + BLOCK]
            q_cur[r:r + BLOCK] = q_new[r:r + BLOCK]
            ga_cur[r:r + BLOCK] = ga_new[r:r + BLOCK]
            gc_cur[r:r + BLOCK] = gc_new[r:r + BLOCK]

    def init():
        zk = jnp.zeros((BLOCK, LANES), BF16)
        ones = jnp.ones((V_ROWS - HEAD_DIM, T + BLOCK), BF16)
        for g in range(N_KV_HEADS):
            ke_cur[g, T:T + BLOCK] = zk
            ko_cur[g, T:T + BLOCK] = zk
            vt_cur[g, 0:HEAD_DIM, T:T + BLOCK] = jnp.zeros((HEAD_DIM, BLOCK), BF16)
            vt_cur[g, HEAD_DIM:V_ROWS, :] = ones
            vt_new[g, HEAD_DIM:V_ROWS, :] = ones[:, :T]
        u_cur[T:T + HALO_U] = jnp.zeros((HALO_U, W_CONV), F32)

    HALF = 2 * LANES

    def proj(c0, n):
        return jnp.dot(h_s[...], win_ref[:, c0:c0 + n], preferred_element_type=F32)

    def p_norm():
        h_s[...] = _rms_rows(x_ref[0], nw_ref[...]).astype(BF16)

    def p_q(c):
        c0 = c * HALF
        q = _head_rms(proj(COL_Q + c0, HALF), g_ref[:HALF, :HALF], qw_ref[:, c0:c0 + HALF])
        q_new[:, c0:c0 + HALF] = q.astype(BF16)

    def p_kv():
        kv = proj(COL_KV, 2 * W_KV)
        kn = _head_rms(kv[:, :W_KV], g_ref[:W_KV, :W_KV], kw_ref[...])
        even, odd = _kv_split(kn)
        vt = kv[:, W_KV:].T.astype(BF16)
        for g in range(N_KV_HEADS):
            ke_new[g] = even[g].astype(BF16)
            ko_new[g] = odd[g].astype(BF16)
            vt_new[g, 0:HEAD_DIM] = vt[g * HEAD_DIM:(g + 1) * HEAD_DIM]

    def p_gate_att(c):
        c0 = c * HALF
        ga_new[:, c0:c0 + HALF] = jax.nn.silu(proj(COL_GATT + c0, HALF))

    def p_glu(c):
        c0 = c * HALF
        val = proj(COL_CONV + c0, HALF)
        gate = proj(COL_CONV + W_CONV + c0, HALF)
        u_new[:, c0:c0 + HALF] = val * jax.nn.sigmoid(gate)

    def p_gate_conv(c):
        c0 = c * HALF
        gc_new[:, c0:c0 + HALF] = jax.nn.silu(proj(COL_GCONV + c0, HALF))

    def attend(n, g):
        r0 = n * BLOCK
        var = 1
        if n == nblk - 1:
            var = jnp.where(seq_last, 2, var)
        if n == 0:
            var = jnp.where(seq_first, 0, var)
        if n == nblk - 1:
            kwin_e = [ke_cur[g, r0:r0 + 2 * BLOCK], ke_new[g, 0:BLOCK]]
            kwin_o = [ko_cur[g, r0:r0 + 2 * BLOCK], ko_new[g, 0:BLOCK]]
            vwin = [vt_cur[g, :, r0:r0 + 2 * BLOCK], vt_new[g, :, 0:BLOCK]]
        else:
            kwin_e = [ke_cur[g, r0:r0 + N_WIN]]
            kwin_o = [ko_cur[g, r0:r0 + N_WIN]]
            vwin = [vt_cur[g, :, r0:r0 + N_WIN]]
        kp = jnp.concatenate(kwin_e + [kme_ref[g]] + kwin_o + [kmo_ref[g]], axis=0)
        vp = jnp.concatenate(vwin + [vmt_ref[g]], axis=1)
        c0 = g * HALF
        q2 = jnp.concatenate([q_cur[r0:r0 + BLOCK, c0:c0 + LANES],
                              q_cur[r0:r0 + BLOCK, c0 + LANES:c0 + HALF]], axis=0)
        s = lax.dot_general(kp, q2, (((1,), (1,)), ((), ())), preferred_element_type=F32)
        s = s + bias_ref[var, g]
        halves = []
        for eo in range(2):
            se = s[eo * N_KEYS:(eo + 1) * N_KEYS]
            sk = sink_ref[g, eo:eo + 1, :]
            m = jnp.maximum(jnp.max(se, axis=0, keepdims=True), sk)
            p = jnp.exp(se - m).astype(BF16)
            o = jnp.dot(vp, p, preferred_element_type=F32)
            den = o[HEAD_DIM:HEAD_DIM + 1] + jnp.exp(sk - m)
            halves.append(o[0:HEAD_DIM] * (1.0 / den))
        att_t = jnp.concatenate([halves[0][:, :LANES], halves[1][:, :LANES],
                                 halves[0][:, LANES:], halves[1][:, LANES:]], axis=0)
        mixed_s[r0:r0 + BLOCK, c0:c0 + HALF] = (att_t.T * ga_cur[r0:r0 + BLOCK, c0:c0 + HALF]).astype(BF16)

    def conv_unit(r):
        h0 = r * CONV_ROWS
        cols = []
        for c0 in range(0, W_CONV, LANES):
            if h0 + CONV_SRC_ROWS <= HALO_U + T:
                src = u_cur[h0:h0 + CONV_SRC_ROWS, c0:c0 + LANES]
            else:
                right = jnp.where(seq_last, 0.0, u_new[0:HALO_U, c0:c0 + LANES])
                src = jnp.concatenate([u_cur[h0:HALO_U + T, c0:c0 + LANES], right], axis=0)
            acc = jnp.broadcast_to(cb_ref[:, c0:c0 + LANES], (CONV_ROWS, LANES))
            for s_ in range(SUBLANES):
                sh = src if s_ == 0 else pltpu.roll(src, CONV_SRC_ROWS - s_, axis=0)
                for a in range(CONV_SRC_ROWS // SUBLANES - CONV_ROWS // SUBLANES):
                    k = SUBLANES * a + s_ - 1
                    if 0 <= k < CONV_K:
                        acc = acc + cw_ref[k:k + 1, c0:c0 + LANES] * sh[SUBLANES * a:SUBLANES * a + CONV_ROWS]
            cols.append(acc)
        y_s[h0:h0 + CONV_ROWS] = jnp.concatenate(cols, axis=1)

    def conv_norm(n):
        r0 = n * BLOCK
        y = y_s[r0:r0 + BLOCK]
        mu = jnp.mean(y, axis=-1, keepdims=True)
        yc = y - mu
        var_y = jnp.mean(yc * yc, axis=-1, keepdims=True)
        yn = yc * lax.rsqrt(var_y + LN_EPS) * lnw_ref[...] + lnb_ref[...]
        mixed_s[r0:r0 + BLOCK, W_ATT:D_MIX] = (jax.nn.silu(yn) * gc_cur[r0:r0 + BLOCK]).astype(BF16)

    def out_proj(n):
        r0 = n * BLOCK
        out_ref[0, r0:r0 + BLOCK] = xres_ref[0, r0:r0 + BLOCK] + jnp.dot(
            mixed_s[r0:r0 + BLOCK], wout_ref[...], preferred_element_type=F32)

    assert nblk == 4 and BLOCK == 2 * CONV_ROWS
    NEW, CUR = 0, 1
    trace = [
        (NEW, p_norm), (CUR, lambda: attend(0, 0)), (CUR, lambda: conv_unit(0)),
        (NEW, lambda: p_q(0)), (CUR, lambda: attend(0, 1)), (CUR, lambda: conv_unit(1)),
        (NEW, lambda: p_q(1)), (CUR, lambda: conv_norm(0)), (CUR, lambda: attend(1, 0)), (CUR, lambda: conv_unit(2)),
        (NEW, p_kv), (CUR, lambda: out_proj(0)), (CUR, lambda: attend(1, 1)), (CUR, lambda: conv_unit(3)),
        (NEW, lambda: p_gate_att(0)), (CUR, lambda: conv_norm(1)), (CUR, lambda: attend(2, 0)), (CUR, lambda: conv_unit(4)),
        (NEW, lambda: p_gate_att(1)), (CUR, lambda: out_proj(1)), (CUR, lambda: attend(2, 1)), (CUR, lambda: conv_unit(5)),
        (NEW, lambda: p_glu(0)), (CUR, lambda: conv_norm(2)), (CUR, lambda: attend(3, 0)), (CUR, lambda: conv_unit(6)),
        (NEW, lambda: p_glu(1)), (CUR, lambda: out_proj(2)), (CUR, lambda: attend(3, 1)), (CUR, lambda: conv_unit(7)),
        (NEW, lambda: p_gate_conv(0)), (CUR, lambda: conv_norm(3)),
        (NEW, lambda: p_gate_conv(1)), (CUR, lambda: out_proj(3)),
    ]

    has_new = i < n_tiles
    has_cur = i >= 1

    @pl.when(i == 0)
    def _():
        init()
        for kind, item in trace:
            if kind == NEW:
                item()

    @pl.when(jnp.logical_and(has_new, has_cur))
    def _():
        shift()
        for _, item in trace:
            item()

    @pl.when(i == n_tiles)
    def _():
        shift()
        for kind, item in trace:
            if kind == CUR:
                item()


def _bias_table():
    r = np.arange(2 * N_KEYS)
    c = np.arange(2 * LANES)
    eo = (r // N_KEYS)[:, None]
    kk = (r % N_KEYS)[:, None]
    lane_tile = (c // LANES)[None, :]
    qi = (c % LANES)[None, :]
    rel = np.abs(kk - BLOCK - qi).astype(np.float32)
    out = np.zeros((3, N_KV_HEADS, 2 * N_KEYS, 2 * LANES), np.float32)
    for g in range(N_KV_HEADS):
        head = 4 * g + 2 * lane_tile + eo
        slope = np.exp2(-8.0 * (head + 1) / N_HEADS).astype(np.float32)
        base = np.where(rel <= BLOCK, -slope * rel, -np.inf).astype(np.float32)
        base = np.where(kk >= N_WIN, np.float32(0.0), base)
        out[1, g] = base
        out[0, g] = np.where(kk < BLOCK, -np.inf, base)
        out[2, g] = np.where((kk >= 2 * BLOCK) & (kk < N_WIN), -np.inf, base)
    return out


def _const_spec(shape):
    nd = len(shape)
    return pl.BlockSpec(shape, lambda i, _nd=nd: (0,) * _nd, pipeline_mode=pl.Buffered(1))


def _run_stream(x, consts, *, tile):
    B, S, D = x.shape
    assert D == D_MODEL and S % tile == 0 and tile % BLOCK == 0 and S // tile >= 2
    tiles_per_seq = S // tile
    n_tiles = B * tiles_per_seq
    T = tile
    xt = x.reshape(n_tiles, T, D)
    body = functools.partial(_main_kernel, tile=tile, tiles_per_seq=tiles_per_seq, n_tiles=n_tiles)
    x_spec = pl.BlockSpec((1, T, D), lambda i: (jnp.minimum(i, n_tiles - 1), 0, 0))
    lag_map = lambda i: (jnp.maximum(i - 1, 0), 0, 0)
    in_specs = [x_spec, pl.BlockSpec((1, T, D), lag_map)] + [_const_spec(c.shape) for c in consts]
    scratch = [
        pltpu.VMEM((T, D_MODEL), BF16),
        pltpu.VMEM((T, W_ATT), BF16),
        pltpu.VMEM((T, W_ATT), BF16),
        pltpu.VMEM((N_KV_HEADS, BLOCK + T, LANES), BF16),
        pltpu.VMEM((N_KV_HEADS, T, LANES), BF16),
        pltpu.VMEM((N_KV_HEADS, BLOCK + T, LANES), BF16),
        pltpu.VMEM((N_KV_HEADS, T, LANES), BF16),
        pltpu.VMEM((N_KV_HEADS, V_ROWS, BLOCK + T), BF16),
        pltpu.VMEM((N_KV_HEADS, V_ROWS, T), BF16),
        pltpu.VMEM((T, W_ATT), F32),
        pltpu.VMEM((T, W_ATT), F32),
        pltpu.VMEM((T, W_CONV), F32),
        pltpu.VMEM((T, W_CONV), F32),
        pltpu.VMEM((HALO_U + T, W_CONV), F32),
        pltpu.VMEM((T, W_CONV), F32),
        pltpu.VMEM((T, W_CONV), F32),
        pltpu.VMEM((T, D_MIX), BF16),
    ]
    out = pl.pallas_call(
        body,
        grid=(n_tiles + 1,),
        in_specs=in_specs,
        out_specs=pl.BlockSpec((1, T, D), lag_map),
        out_shape=jax.ShapeDtypeStruct((n_tiles, T, D), F32),
        scratch_shapes=scratch,
        compiler_params=pltpu.CompilerParams(
            dimension_semantics=("arbitrary",),
            vmem_limit_bytes=VMEM_LIMIT_BYTES),
        name="hybrid_block",
    )(xt, xt, *consts)
    return out.reshape(B, S, D)


def kernel(x_prompt, x_sample, meta_tokens, norm_w, w_in, q_norm_w, k_norm_w, sink_logits,
           conv_w, conv_b, conv_ln_w, conv_ln_b, w_out):
    nw = norm_w[0].reshape(1, D_MODEL)
    win = w_in[0].astype(BF16)
    wout = w_out[0].astype(BF16)
    qw = (jnp.tile(q_norm_w[0], N_HEADS) * (HEAD_DIM ** -0.5)).reshape(1, W_ATT)
    kw = jnp.tile(k_norm_w[0], N_KV_HEADS).reshape(1, W_KV)
    head_id = np.arange(W_ATT) // HEAD_DIM
    gmat = jnp.asarray((head_id[:, None] == head_id[None, :]).astype(np.float32) / HEAD_DIM, BF16)

    kn_m, v_m, u_m = pl.pallas_call(
        _meta_kernel,
        out_shape=(jax.ShapeDtypeStruct((N_META, W_KV), F32),
                   jax.ShapeDtypeStruct((N_META, W_KV), F32),
                   jax.ShapeDtypeStruct((N_META, W_CONV), F32)),
        name="meta_tokens",
    )(meta_tokens, nw, win, gmat, kw)

    zeros = jnp.zeros((N_META, HEAD_DIM), F32)
    k_heads = [kn_m[:, :HEAD_DIM], kn_m[:, HEAD_DIM:]]
    kme = jnp.stack([jnp.concatenate([k, zeros], axis=1) for k in k_heads]).astype(BF16)
    kmo = jnp.stack([jnp.concatenate([zeros, k], axis=1) for k in k_heads]).astype(BF16)
    ones = jnp.ones((V_ROWS - HEAD_DIM, N_META), F32)
    vmt = jnp.stack([jnp.concatenate([v_m[:, g * HEAD_DIM:(g + 1) * HEAD_DIM].T, ones], axis=0)
                     for g in range(N_KV_HEADS)]).astype(BF16)

    lane_tile = np.arange(2 * LANES) // LANES
    head_idx = np.stack([np.stack([4 * g + 2 * lane_tile + eo for eo in range(2)]) for g in range(N_KV_HEADS)])
    sink_rows = sink_logits[0][head_idx]

    cw = jnp.pad(conv_w[0], ((0, 1), (0, 0)))
    consts = (nw, win, gmat, qw, kw, kme, kmo, vmt, u_m, jnp.asarray(_bias_table()), sink_rows,
              cw, conv_b[0].reshape(1, W_CONV), conv_ln_w[0].reshape(1, W_CONV),
              conv_ln_b[0].reshape(1, W_CONV), wout)
    y_prompt = _run_stream(x_prompt, consts, tile=TILE)
    y_sample = _run_stream(x_sample, consts, tile=TILE)
    return (y_prompt, y_sample)
```
